```python
import jax, jax.numpy as jnp
from jax import lax
import numpy as np

D_MODEL = 1024
BATCH = 8
SEQ = 4096
DEPTH = 2
DEC_BATCH = 8
DEC_SEQ = 32
PAST_LEN = 4096

CHUNK = 64
MLP_CHUNK = 128
W_A = D_MODEL
HEADS_A = 8
HEAD_DIM_A = W_A // HEADS_A
W_B = D_MODEL
POOL_WINDOWS = (2, 4, 8, 16)
N_POOL_GROUPS = len(POOL_WINDOWS)
POOL_GROUP_DIM = W_B // N_POOL_GROUPS
POOL_STATE = max(POOL_WINDOWS) - 1
D_MIX = W_A + W_B
D_IN_PROJ = 3 * W_A + 2 * W_B
EPS = 1e-6

kernel_name = "hybrid_gmlp_pool_stream_step"


def rmsnorm(x, g):
    xf = x.astype(jnp.float32)
    y = xf * lax.rsqrt(jnp.mean(xf * xf, axis=-1, keepdims=True) + EPS)
    return (y * g.astype(jnp.float32)).astype(x.dtype)


def layernorm(x, g, b):
    xf = x.astype(jnp.float32)
    mu = jnp.mean(xf, axis=-1, keepdims=True)
    xc = xf - mu
    y = xc * lax.rsqrt(jnp.mean(xc * xc, axis=-1, keepdims=True) + EPS)
    return (y * g.astype(jnp.float32) + b.astype(jnp.float32)).astype(x.dtype)


def chunk_mlp(v, w_spatial, b_spatial):
    B, T, _ = v.shape
    pad = (-T) % MLP_CHUNK
    vp = jnp.pad(v, ((0, 0), (0, pad), (0, 0)))
    n = (T + pad) // MLP_CHUNK
    vr = vp.reshape(B, n, MLP_CHUNK, HEADS_A, HEAD_DIM_A)
    blk = jnp.arange(MLP_CHUNK) // CHUNK
    mask = blk[None, :] <= blk[:, None]
    w = jnp.where(mask[None], w_spatial, jnp.zeros_like(w_spatial))
    out = jnp.einsum('hij,bnjhc->bnihc', w, vr) + b_spatial.T[None, None, :, :, None]
    return out.reshape(B, n * MLP_CHUNK, W_A)[:, :T]


def pool_mix(x_ext, pos0, w_pool, pool_scale):
    B, L, _ = x_ext.shape
    T = L - POOL_STATE
    xf = x_ext.astype(jnp.float32)
    cs = jnp.concatenate([jnp.zeros((B, 1, W_B), jnp.float32), jnp.cumsum(xf, axis=1)], axis=1)
    x_new = xf[:, POOL_STATE:]
    pos = jnp.arange(T, dtype=jnp.float32) + pos0
    parts = []
    for g, w in enumerate(POOL_WINDOWS):
        sl = slice(g * POOL_GROUP_DIM, (g + 1) * POOL_GROUP_DIM)
        s = cs[:, POOL_STATE + 1:, sl] - cs[:, POOL_STATE + 1 - w:POOL_STATE + 1 - w + T, sl]
        cnt = jnp.minimum(pos + 1.0, float(w))[None, :, None]
        parts.append(s / cnt - x_new[..., sl])
    d = jnp.stack(parts, axis=2)
    out = jnp.einsum('btgc,gcd->btgd', d, w_pool.astype(jnp.float32)).reshape(B, T, W_B)
    return (out * pool_scale.astype(jnp.float32)).astype(x_ext.dtype)


def mixer_layer(x, pool_past, pos0, norm_g, w_in, ln_v_g, ln_v_b, w_spatial, b_spatial,
                w_pool, pool_scale, w_out):
    h = rmsnorm(x, norm_g)
    proj = jnp.einsum('btd,de->bte', h, w_in)
    u, v, z_a, xb, z_b = jnp.split(proj, [W_A, 2 * W_A, 3 * W_A, 3 * W_A + W_B], axis=-1)
    v_n = layernorm(v, ln_v_g, ln_v_b)
    a = u * chunk_mlp(v_n, w_spatial, b_spatial) * jax.nn.silu(z_a)
    x_ext = jnp.concatenate([pool_past.astype(xb.dtype), xb], axis=1)
    b = pool_mix(x_ext, pos0, w_pool, pool_scale) * jax.nn.silu(z_b)
    y = x + jnp.einsum('bte,ed->btd', jnp.concatenate([a, b], axis=-1), w_out)
    return y, x_ext[:, -POOL_STATE:], v_n


def setup_inputs(seed: int = 0) -> dict:
    key = jax.random.key(seed)
    ks = jax.random.split(key, 16)
    f32 = jnp.float32
    x_prompt = jax.random.normal(ks[0], (BATCH, SEQ, D_MODEL), f32)
    x_sample = jax.random.normal(ks[1], (DEC_BATCH, DEC_SEQ, D_MODEL), f32)
    state_pool = jax.random.normal(ks[2], (DEPTH, DEC_BATCH, POOL_STATE, W_B), f32)
    norm_g = 1.0 + 0.05 * jax.random.normal(ks[3], (DEPTH, D_MODEL), f32)
    w_in = jax.random.normal(ks[4], (DEPTH, D_MODEL, D_IN_PROJ), f32) * D_MODEL ** -0.5
    ln_v_g = 1.0 + 0.05 * jax.random.normal(ks[5], (DEPTH, W_A), f32)
    ln_v_b = 0.02 * jax.random.normal(ks[6], (DEPTH, W_A), f32)
    w_spatial = jax.random.normal(ks[7], (DEPTH, HEADS_A, MLP_CHUNK, MLP_CHUNK), f32) * MLP_CHUNK ** -0.5
    b_spatial = 1.0 + 0.1 * jax.random.normal(ks[8], (DEPTH, HEADS_A, MLP_CHUNK), f32)
    w_pool = jax.random.normal(ks[9], (DEPTH, N_POOL_GROUPS, POOL_GROUP_DIM, POOL_GROUP_DIM), f32) * POOL_GROUP_DIM ** -0.5
    pool_scale = 1.0 + 0.1 * jax.random.normal(ks[10], (DEPTH, W_B), f32)
    w_out = jax.random.normal(ks[11], (DEPTH, D_MIX, D_MODEL), f32) * D_MIX ** -0.5
    final_g = 1.0 + 0.05 * jax.random.normal(ks[12], (D_MODEL,), f32)
    return {"x_prompt": x_prompt, "x_sample": x_sample, "state_pool": state_pool,
            "norm_g": norm_g, "w_in": w_in, "ln_v_g": ln_v_g, "ln_v_b": ln_v_b,
            "w_spatial": w_spatial, "b_spatial": b_spatial, "w_pool": w_pool,
            "pool_scale": pool_scale, "w_out": w_out, "final_g": final_g}


def reference(x_prompt, x_sample, state_pool, norm_g, w_in, ln_v_g, ln_v_b, w_spatial,
              b_spatial, w_pool, pool_scale, w_out, final_g):
    xp = x_prompt
    xs = x_sample
    pool_p, pool_s, v_s = [], [], []
    zero_past = jnp.zeros((x_prompt.shape[0], POOL_STATE, W_B), x_prompt.dtype)
    for l in range(DEPTH):
        params = (norm_g[l], w_in[l], ln_v_g[l], ln_v_b[l], w_spatial[l], b_spatial[l],
                  w_pool[l], pool_scale[l], w_out[l])
        xp, new_pool_p, _ = mixer_layer(xp, zero_past, 0, *params)
        xs, new_pool_s, v_rows = mixer_layer(xs, state_pool[l], PAST_LEN, *params)
        pool_p.append(new_pool_p)
        pool_s.append(new_pool_s)
        v_s.append(v_rows)
    y_prompt = rmsnorm(xp, final_g)
    y_sample = rmsnorm(xs, final_g)
    new_state_pool_prompt = jnp.stack(pool_p, axis=0)
    new_state_pool_sample = jnp.stack(pool_s, axis=0)
    new_chunk_v_sample = jnp.stack(v_s, axis=0)
    return (y_prompt, y_sample, new_state_pool_prompt, new_state_pool_sample, new_chunk_v_sample)
```

```python
import functools

import jax
import jax.numpy as jnp
from jax import lax
from jax.experimental import pallas as pl
from jax.experimental.pallas import tpu as pltpu

D_MODEL = 1024
W_A = 1024
W_B = 1024
HEADS_A = 8
HEAD_DIM_A = W_A // HEADS_A
CHUNK = 64
MLP_CHUNK = 128
POOL_WINDOWS = (2, 4, 8, 16)
POOL_GROUP_DIM = W_B // len(POOL_WINDOWS)
POOL_STATE = max(POOL_WINDOWS) - 1
EPS = 1e-6

LANES = 128
SUBLANES = 8
HALO = 16
VMEM_LIMIT_BYTES = 58 * 1024 * 1024
PROMPT_TILE = 512

F32 = jnp.float32
BF16 = jnp.bfloat16


def _rmsnorm(x, g):
    ms = jnp.mean(x * x, axis=-1, keepdims=True)
    return x * lax.rsqrt(ms + EPS) * g


def _layernorm(v, g, b):
    mu = jnp.mean(v, axis=-1, keepdims=True)
    vc = v - mu
    var = jnp.mean(vc * vc, axis=-1, keepdims=True)
    return vc * lax.rsqrt(var + EPS) * g + b


def _silu(z):
    return z / (1.0 + jnp.exp(-z))


def _masked_spatial(ws):
    i = lax.broadcasted_iota(jnp.int32, (MLP_CHUNK, MLP_CHUNK), 0)
    j = lax.broadcasted_iota(jnp.int32, (MLP_CHUNK, MLP_CHUNK), 1)
    keep = (i >= CHUNK) | (j < CHUNK)
    return jnp.where(keep[None], ws, 0.0).astype(BF16)


def _pool_delta(xb_ref, r0, rows, inv_cnt):
    parts = []
    for g, w in enumerate(POOL_WINDOWS):
        cols = slice(g * POOL_GROUP_DIM, (g + 1) * POOL_GROUP_DIM)
        x0 = xb_ref[pl.ds(r0, rows), cols]
        s = x0
        for k in range(1, w):
            s = s + xb_ref[pl.ds(r0 - k, rows), cols]
        parts.append(s * inv_cnt[g] - x0)
    return parts


def _group_linear(d_parts, wp_ref):
    outs = [jnp.dot(d.astype(BF16), wp_ref[g], preferred_element_type=F32)
            for g, d in enumerate(d_parts)]
    return jnp.concatenate(outs, axis=-1)


def _project(h, w_in_ref, sec):
    return jnp.dot(h, w_in_ref[:, sec * W_A:(sec + 1) * W_A], preferred_element_type=F32)


def _prompt_layer_kernel(x_ref, ng_ref, w_in_ref, lg_ref, lb_ref, ws_ref, bs_ref, wp_ref,
                         ps_ref, w_out_ref, fg_ref, y_ref, pool_ref, xb_ref, a_ref, *,
                         tile, final_norm):
    j = pl.program_id(1)

    @pl.when(j == 0)
    def _():
        xb_ref[0:HALO, :] = jnp.zeros((HALO, W_B), F32)

    x = x_ref[...]
    h = _rmsnorm(x, ng_ref[...]).astype(BF16)

    vn = _layernorm(_project(h, w_in_ref, 1), lg_ref[...], lb_ref[...]).astype(BF16)
    u = _project(h, w_in_ref, 0)
    gate_a = _silu(_project(h, w_in_ref, 2))
    wm = _masked_spatial(ws_ref[...])
    bs = bs_ref[...]
    for c in range(tile // MLP_CHUNK):
        rs = slice(c * MLP_CHUNK, (c + 1) * MLP_CHUNK)
        for hd in range(HEADS_A):
            cs = slice(hd * HEAD_DIM_A, (hd + 1) * HEAD_DIM_A)
            sp = jnp.dot(wm[hd], vn[rs, cs], preferred_element_type=F32) + bs[:, hd:hd + 1]
            a_ref[rs, cs] = (u[rs, cs] * sp * gate_a[rs, cs]).astype(BF16)

    xb_ref[HALO:HALO + tile, :] = _project(h, w_in_ref, 3)
    pool_ref[...] = xb_ref[pl.ds(HALO + tile - POOL_STATE, POOL_STATE), :]
    pos1 = lax.broadcasted_iota(jnp.int32, (tile, LANES), 0) + (j * tile + 1)
    inv_cnt = []
    for w in POOL_WINDOWS:
        inv = 1.0 / jnp.minimum(pos1, w).astype(F32)
        inv_cnt.append(jnp.concatenate([inv] * (POOL_GROUP_DIM // LANES), axis=-1))
    pm = _group_linear(_pool_delta(xb_ref, HALO, tile, inv_cnt), wp_ref)
    xb_ref[0:HALO, :] = xb_ref[tile:tile + HALO, :]
    bb = (pm * ps_ref[...] * _silu(_project(h, w_in_ref, 4))).astype(BF16)

    y = x + (jnp.dot(a_ref[...], w_out_ref[0:W_A, :], preferred_element_type=F32)
             + jnp.dot(bb, w_out_ref[W_A:W_A + W_B, :], preferred_element_type=F32))
    if final_norm:
        y = _rmsnorm(y, fg_ref[...])
    y_ref[...] = y


def _sample_layer_kernel(x_ref, st_ref, ng_ref, w_in_ref, lg_ref, lb_ref, ws_ref, bs_ref, wp_ref,
                         ps_ref, w_out_ref, fg_ref, y_ref, pool_ref, vn_ref, xb_ref, *,
                         n_batch, seq, final_norm):
    x = x_ref[...]
    h = _rmsnorm(x, ng_ref[...]).astype(BF16)

    vn = _layernorm(_project(h, w_in_ref, 1), lg_ref[...], lb_ref[...])
    vn_ref[...] = vn
    u = _project(h, w_in_ref, 0)
    gate_a = _silu(_project(h, w_in_ref, 2))
    wm = _masked_spatial(ws_ref[...])
    bs = bs_ref[...]
    pad = jnp.zeros((MLP_CHUNK - seq, W_A), F32)
    sp_rows = []
    for b in range(n_batch):
        vb = jnp.concatenate([vn[b * seq:(b + 1) * seq, :], pad], axis=0).astype(BF16)
        heads = []
        for hd in range(HEADS_A):
            cs = slice(hd * HEAD_DIM_A, (hd + 1) * HEAD_DIM_A)
            sp = jnp.dot(wm[hd], vb[:, cs], preferred_element_type=F32) + bs[:, hd:hd + 1]
            heads.append(sp[0:seq, :])
        sp_rows.append(jnp.concatenate(heads, axis=-1))
    a = (u * jnp.concatenate(sp_rows, axis=0) * gate_a).astype(BF16)

    xb = _project(h, w_in_ref, 3)
    stride = HALO + seq
    inv_cnt = [1.0 / w for w in POOL_WINDOWS]
    d_rows = []
    for b in range(n_batch):
        base = b * stride
        xb_ref[base + HALO - POOL_STATE:base + HALO, :] = st_ref[b]
        xb_ref[base + HALO:base + stride, :] = xb[b * seq:(b + 1) * seq, :]
        pool_ref[b] = xb_ref[pl.ds(base + stride - POOL_STATE, POOL_STATE), :]
        d_rows.append(_pool_delta(xb_ref, base + HALO, seq, inv_cnt))
    d_parts = [jnp.concatenate([d_rows[b][g] for b in range(n_batch)], axis=0)
               for g in range(len(POOL_WINDOWS))]
    pm = _group_linear(d_parts, wp_ref)
    bb = (pm * ps_ref[...] * _silu(_project(h, w_in_ref, 4))).astype(BF16)

    y = x + (jnp.dot(a, w_out_ref[0:W_A, :], preferred_element_type=F32)
             + jnp.dot(bb, w_out_ref[W_A:W_A + W_B, :], preferred_element_type=F32))
    if final_norm:
        y = _rmsnorm(y, fg_ref[...])
    y_ref[...] = y


def _resident(shape):
    zeros = (0,) * len(shape)
    return pl.BlockSpec(shape, lambda b, j: zeros, pipeline_mode=pl.Buffered(1))


def _prompt_layer(x, params, final_norm):
    n_batch, seq, d = x.shape
    tile = PROMPT_TILE
    assert seq % tile == 0 and tile % MLP_CHUNK == 0 and tile >= HALO
    row_spec = pl.BlockSpec((None, tile, d), lambda b, j: (b, j, 0))
    in_specs = [row_spec] + [_resident(p.shape) for p in params]
    out_specs = [row_spec, pl.BlockSpec((None, POOL_STATE, W_B), lambda b, j: (b, 0, 0))]
    out_shape = [jax.ShapeDtypeStruct((n_batch, seq, d), F32),
                 jax.ShapeDtypeStruct((n_batch, POOL_STATE, W_B), F32)]
    return pl.pallas_call(
        functools.partial(_prompt_layer_kernel, tile=tile, final_norm=final_norm),
        grid=(n_batch, seq // tile),
        in_specs=in_specs,
        out_specs=out_specs,
        out_shape=out_shape,
        scratch_shapes=[pltpu.VMEM((HALO + tile, W_B), F32), pltpu.VMEM((tile, W_A), BF16)],
        compiler_params=pltpu.CompilerParams(
            dimension_semantics=("arbitrary", "arbitrary"),
            vmem_limit_bytes=VMEM_LIMIT_BYTES),
        name="prompt_layer",
    )(x, *params)


def _sample_layer(x, state, params, final_norm):
    n_batch, seq, d = x.shape
    assert POOL_STATE <= seq <= MLP_CHUNK and seq % SUBLANES == 0
    rows = n_batch * seq
    out_shape = [jax.ShapeDtypeStruct((rows, d), F32),
                 jax.ShapeDtypeStruct((n_batch, POOL_STATE, W_B), F32),
                 jax.ShapeDtypeStruct((rows, W_A), F32)]
    y, pool, vn = pl.pallas_call(
        functools.partial(_sample_layer_kernel, n_batch=n_batch, seq=seq, final_norm=final_norm),
        out_shape=out_shape,
        scratch_shapes=[pltpu.VMEM((n_batch * (HALO + seq), W_B), F32)],
        compiler_params=pltpu.CompilerParams(vmem_limit_bytes=VMEM_LIMIT_BYTES),
        name="sample_layer",
    )(x.reshape(rows, d), state, *params)
    return y.reshape(n_batch, seq, d), pool, vn.reshape(n_batch, seq, W_A)


def kernel(x_prompt, x_sample, state_pool, norm_g, w_in, ln_v_g, ln_v_b, w_spatial, b_spatial,
           w_pool, pool_scale, w_out, final_g):
    depth = w_in.shape[0]
    w_in_b = w_in.astype(BF16)
    w_pool_b = w_pool.astype(BF16)
    w_out_b = w_out.astype(BF16)
    fg = final_g.reshape(1, D_MODEL)
    xp, xs = x_prompt, x_sample
    pool_p, pool_s, v_s = [], [], []
    for l in range(depth):
        params = (norm_g[l].reshape(1, D_MODEL), w_in_b[l], ln_v_g[l].reshape(1, W_A),
                  ln_v_b[l].reshape(1, W_A), w_spatial[l], b_spatial[l].T, w_pool_b[l],
                  pool_scale[l].reshape(1, W_B), w_out_b[l], fg)
        last = l == depth - 1
        xp, new_pool_p = _prompt_layer(xp, params, last)
        xs, new_pool_s, v_rows = _sample_layer(xs, state_pool[l], params, last)
        pool_p.append(new_pool_p)
        pool_s.append(new_pool_s)
        v_s.append(v_rows)
    return (xp, xs, jnp.stack(pool_p, axis=0), jnp.stack(pool_s, axis=0), jnp.stack(v_s, axis=0))
```

```python
import functools

import jax
import jax.numpy as jnp
from jax import lax
from jax.experimental import pallas as pl
from jax.experimental.pallas import tpu as pltpu

D_MODEL = 1024
W_A = 1024
W_B = 1024
HEADS_A = 8
HEAD_DIM_A = W_A // HEADS_A
CHUNK = 64
MLP_CHUNK = 128
POOL_WINDOWS = (2, 4, 8, 16)
POOL_GROUP_DIM = W_B // len(POOL_WINDOWS)
POOL_STATE = max(POOL_WINDOWS) - 1
EPS = 1e-6

LANES = 128
SUBLANES = 8
HALO = 16
VMEM_LIMIT_BYTES = 58 * 1024 * 1024
PROMPT_TILE = 512

F32 = jnp.float32
BF16 = jnp.bfloat16


def _rmsnorm(x, g):
    ms = jnp.mean(x * x, axis=-1, keepdims=True)
    return x * lax.rsqrt(ms + EPS) * g


def _layernorm(v, g, b):
    mu = jnp.mean(v, axis=-1, keepdims=True)
    vc = v - mu
    var = jnp.mean(vc * vc, axis=-1, keepdims=True)
    return vc * lax.rsqrt(var + EPS) * g + b


def _silu(z):
    return z / (1.0 + jnp.exp(-z))


def _masked_spatial(ws):
    i = lax.broadcasted_iota(jnp.int32, (MLP_CHUNK, MLP_CHUNK), 0)
    j = lax.broadcasted_iota(jnp.int32, (MLP_CHUNK, MLP_CHUNK), 1)
    keep = (i >= CHUNK) | (j < CHUNK)
    return jnp.where(keep[None], ws, 0.0).astype(BF16)


def _pool_delta(xb_ref, r0, rows, inv_cnt):
    parts = []
    for g, w in enumerate(POOL_WINDOWS):
        cols = slice(g * POOL_GROUP_DIM, (g + 1) * POOL_GROUP_DIM)
        xe = xb_ref[pl.ds(r0 - HALO, HALO + rows), cols]
        s = xe
        span = 1
        while span < w:
            s = s + pltpu.roll(s, span, axis=0)
            span *= 2
        parts.append(s[HALO:, :] * inv_cnt[g] - xe[HALO:, :])
    return parts


def _group_linear(d_parts, wp_ref):
    outs = [jnp.dot(d.astype(BF16), wp_ref[g], preferred_element_type=F32)
            for g, d in enumerate(d_parts)]
    return jnp.concatenate(outs, axis=-1)


def _project(h, w_in_ref, sec):
    return jnp.dot(h, w_in_ref[:, sec * W_A:(sec + 1) * W_A], preferred_element_type=F32)


def _prompt_layer_kernel(x_ref, ng_ref, w_in_ref, lg_ref, lb_ref, ws_ref, bs_ref, wp_ref,
                         ps_ref, w_out_ref, fg_ref, y_ref, pool_ref, xb_ref, a_ref, *,
                         tile, final_norm):
    j = pl.program_id(1)

    @pl.when(j == 0)
    def _():
        xb_ref[0:HALO, :] = jnp.zeros((HALO, W_B), F32)

    x = x_ref[...]
    h = _rmsnorm(x, ng_ref[...]).astype(BF16)

    v = _project(h, w_in_ref, 1)
    xb_ref[HALO:HALO + tile, :] = _project(h, w_in_ref, 3)

    vn = _layernorm(v, lg_ref[...], lb_ref[...]).astype(BF16)
    pool_ref[...] = xb_ref[pl.ds(HALO + tile - POOL_STATE, POOL_STATE), :]
    pos1 = lax.broadcasted_iota(jnp.int32, (tile, LANES), 0) + (j * tile + 1)
    inv_cnt = []
    for w in POOL_WINDOWS:
        inv = 1.0 / jnp.minimum(pos1, w).astype(F32)
        inv_cnt.append(jnp.concatenate([inv] * (POOL_GROUP_DIM // LANES), axis=-1))
    d_parts = _pool_delta(xb_ref, HALO, tile, inv_cnt)
    xb_ref[0:HALO, :] = xb_ref[tile:tile + HALO, :]

    u = _project(h, w_in_ref, 0)
    gate_a = _silu(_project(h, w_in_ref, 2))
    wm = _masked_spatial(ws_ref[...])
    bs = bs_ref[...]
    for c in range(tile // MLP_CHUNK):
        rs = slice(c * MLP_CHUNK, (c + 1) * MLP_CHUNK)
        for hd in range(HEADS_A):
            cs = slice(hd * HEAD_DIM_A, (hd + 1) * HEAD_DIM_A)
            sp = jnp.dot(wm[hd], vn[rs, cs], preferred_element_type=F32) + bs[:, hd:hd + 1]
            a_ref[rs, cs] = (u[rs, cs] * sp * gate_a[rs, cs]).astype(BF16)

    gate_b = _silu(_project(h, w_in_ref, 4))
    bb = (_group_linear(d_parts, wp_ref) * ps_ref[...] * gate_b).astype(BF16)

    y = x + (jnp.dot(a_ref[...], w_out_ref[0:W_A, :], preferred_element_type=F32)
             + jnp.dot(bb, w_out_ref[W_A:W_A + W_B, :], preferred_element_type=F32))
    if final_norm:
        y = _rmsnorm(y, fg_ref[...])
    y_ref[...] = y


def _sample_layer_kernel(x_ref, st_ref, ng_ref, w_in_ref, lg_ref, lb_ref, ws_ref, bs_ref, wp_ref,
                         ps_ref, w_out_ref, fg_ref, y_ref, pool_ref, vn_ref, xb_ref, *,
                         n_batch, seq, final_norm):
    x = x_ref[...]
    h = _rmsnorm(x, ng_ref[...]).astype(BF16)

    vn = _layernorm(_project(h, w_in_ref, 1), lg_ref[...], lb_ref[...])
    vn_ref[...] = vn
    u = _project(h, w_in_ref, 0)
    gate_a = _silu(_project(h, w_in_ref, 2))
    wm = _masked_spatial(ws_ref[...])
    bs = bs_ref[...]
    pad = jnp.zeros((MLP_CHUNK - seq, W_A), F32)
    sp_rows = []
    for b in range(n_batch):
        vb = jnp.concatenate([vn[b * seq:(b + 1) * seq, :], pad], axis=0).astype(BF16)
        heads = []
        for hd in range(HEADS_A):
            cs = slice(hd * HEAD_DIM_A, (hd + 1) * HEAD_DIM_A)
            sp = jnp.dot(wm[hd], vb[:, cs], preferred_element_type=F32) + bs[:, hd:hd + 1]
            heads.append(sp[0:seq, :])
        sp_rows.append(jnp.concatenate(heads, axis=-1))
    a = (u * jnp.concatenate(sp_rows, axis=0) * gate_a).astype(BF16)

    xb = _project(h, w_in_ref, 3)
    stride = HALO + seq
    inv_cnt = [1.0 / w for w in POOL_WINDOWS]
    d_rows = []
    for b in range(n_batch):
        base = b * stride
        xb_ref[base + HALO - POOL_STATE:base + HALO, :] = st_ref[b]
        xb_ref[base + HALO:base + stride, :] = xb[b * seq:(b + 1) * seq, :]
        pool_ref[b] = xb_ref[pl.ds(base + stride - POOL_STATE, POOL_STATE), :]
        d_rows.append(_pool_delta(xb_ref, base + HALO, seq, inv_cnt))
    d_parts = [jnp.concatenate([d_rows[b][g] for b in range(n_batch)], axis=0)
               for g in range(len(POOL_WINDOWS))]
    pm = _group_linear(d_parts, wp_ref)
    bb = (pm * ps_ref[...] * _silu(_project(h, w_in_ref, 4))).astype(BF16)

    y = x + (jnp.dot(a, w_out_ref[0:W_A, :], preferred_element_type=F32)
             + jnp.dot(bb, w_out_ref[W_A:W_A + W_B, :], preferred_element_type=F32))
    if final_norm:
        y = _rmsnorm(y, fg_ref[...])
    y_ref[...] = y


def _resident(shape):
    zeros = (0,) * len(shape)
    return pl.BlockSpec(shape, lambda b, j: zeros, pipeline_mode=pl.Buffered(1))


def _prompt_layer(x, params, final_norm):
    n_batch, seq, d = x.shape
    tile = PROMPT_TILE
    assert seq % tile == 0 and tile % MLP_CHUNK == 0 and tile >= HALO
    row_spec = pl.BlockSpec((None, tile, d), lambda b, j: (b, j, 0))
    in_specs = [row_spec] + [_resident(p.shape) for p in params]
    out_specs = [row_spec, pl.BlockSpec((None, POOL_STATE, W_B), lambda b, j: (b, 0, 0))]
    out_shape = [jax.ShapeDtypeStruct((n_batch, seq, d), F32),
                 jax.ShapeDtypeStruct((n_batch, POOL_STATE, W_B), F32)]
    return pl.pallas_call(
        functools.partial(_prompt_layer_kernel, tile=tile, final_norm=final_norm),
        grid=(n_batch, seq // tile),
        in_specs=in_specs,
        out_specs=out_specs,
        out_shape=out_shape,
        scratch_shapes=[pltpu.VMEM((HALO + tile, W_B), F32), pltpu.VMEM((tile, W_A), BF16)],
        compiler_params=pltpu.CompilerParams(
            dimension_semantics=("arbitrary", "arbitrary"),
            vmem_limit_bytes=VMEM_LIMIT_BYTES),
        name="prompt_layer",
    )(x, *params)


def _sample_layer(x, state, params, final_norm):
    n_batch, seq, d = x.shape
    assert POOL_STATE <= seq <= MLP_CHUNK and seq % SUBLANES == 0
    rows = n_batch * seq
    out_shape = [jax.ShapeDtypeStruct((rows, d), F32),
                 jax.ShapeDtypeStruct((n_batch, POOL_STATE, W_B), F32),
                 jax.ShapeDtypeStruct((rows, W_A), F32)]
    y, pool, vn = pl.pallas_call(
        functools.partial(_sample_layer_kernel, n_batch=n_batch, seq=seq, final_norm=final_norm),
        out_shape=out_shape,
        scratch_shapes=[pltpu.VMEM((n_batch * (HALO + seq), W_B), F32)],
        compiler_params=pltpu.CompilerParams(vmem_limit_bytes=VMEM_LIMIT_BYTES),
        name="sample_layer",
    )(x.reshape(rows, d), state, *params)
    return y.reshape(n_batch, seq, d), pool, vn.reshape(n_batch, seq, W_A)


def kernel(x_prompt, x_sample, state_pool, norm_g, w_in, ln_v_g, ln_v_b, w_spatial, b_spatial,
           w_pool, pool_scale, w_out, final_g):
    depth = w_in.shape[0]
    w_in_b = w_in.astype(BF16)
    w_pool_b = w_pool.astype(BF16)
    w_out_b = w_out.astype(BF16)
    fg = final_g.reshape(1, D_MODEL)
    xp, xs = x_prompt, x_sample
    pool_p, pool_s, v_s = [], [], []
    for l in range(depth):
        params = (norm_g[l].reshape(1, D_MODEL), w_in_b[l], ln_v_g[l].reshape(1, W_A),
                  ln_v_b[l].reshape(1, W_A), w_spatial[l], b_spatial[l].T, w_pool_b[l],
                  pool_scale[l].reshape(1, W_B), w_out_b[l], fg)
        last = l == depth - 1
        xp, new_pool_p = _prompt_layer(xp, params, last)
        xs, new_pool_s, v_rows = _sample_layer(xs, state_pool[l], params, last)
        pool_p.append(new_pool_p)
        pool_s.append(new_pool_s)
        v_s.append(v_rows)
    return (xp, xs, jnp.stack(pool_p, axis=0), jnp.stack(pool_s, axis=0), jnp.stack(v_s, axis=0))
```

```python
import functools

import jax
import jax.numpy as jnp
from jax import lax
from jax.experimental import pallas as pl
from jax.experimental.pallas import tpu as pltpu

D_MODEL = 1024
W_A = 1024
W_B = 1024
HEADS_A = 8
HEAD_DIM_A = W_A // HEADS_A
CHUNK = 64
MLP_CHUNK = 128
POOL_WINDOWS = (2, 4, 8, 16)
POOL_GROUP_DIM = W_B // len(POOL_WINDOWS)
POOL_STATE = max(POOL_WINDOWS) - 1
PAST_LEN = 4096
EPS = 1e-6

LANES = 128
SUBLANES = 8
HALO = 16
VMEM_LIMIT_BYTES = 58 * 1024 * 1024
PROMPT_TILE = 1024
PROMPT_SUB = 512

F32 = jnp.float32
BF16 = jnp.bfloat16


def _rmsnorm(x, g):
    ms = jnp.mean(x * x, axis=-1, keepdims=True)
    return x * lax.rsqrt(ms + EPS) * g


def _layernorm(v, g, b):
    mu = jnp.mean(v, axis=-1, keepdims=True)
    vc = v - mu
    var = jnp.mean(vc * vc, axis=-1, keepdims=True)
    return vc * lax.rsqrt(var + EPS) * g + b


def _silu(z):
    return z / (1.0 + jnp.exp(-z))


def _masked_spatial(ws):
    i = lax.broadcasted_iota(jnp.int32, (MLP_CHUNK, MLP_CHUNK), 0)
    j = lax.broadcasted_iota(jnp.int32, (MLP_CHUNK, MLP_CHUNK), 1)
    keep = (i >= CHUNK) | (j < CHUNK)
    return jnp.where(keep[None], ws, 0.0).astype(BF16)


def _pool_delta(xb_ref, r0, rows, pos0):
    parts = []
    for g, w in enumerate(POOL_WINDOWS):
        cols = slice(g * POOL_GROUP_DIM, (g + 1) * POOL_GROUP_DIM)
        xe = xb_ref[pl.ds(r0 - HALO, HALO + rows), cols]
        s = xe
        span = 1
        while span < w:
            s = s + pltpu.roll(s, span, axis=0)
            span *= 2
        d = s[HALO:, :] * (1.0 / w) - xe[HALO:, :]
        if not (isinstance(pos0, int) and pos0 >= POOL_STATE):
            pos1 = lax.broadcasted_iota(jnp.int32, (HALO, POOL_GROUP_DIM), 0) + (pos0 + 1)
            inv = 1.0 / jnp.minimum(pos1, w).astype(F32)
            head = s[HALO:2 * HALO, :] * inv - xe[HALO:2 * HALO, :]
            d = jnp.concatenate([head, d[HALO:, :]], axis=0)
        parts.append(d)
    return parts


def _group_linear(d_parts, wp_ref):
    outs = [jnp.dot(d.astype(BF16), wp_ref[g], preferred_element_type=F32)
            for g, d in enumerate(d_parts)]
    return jnp.concatenate(outs, axis=-1)


def _project(h, w_in_ref, sec):
    return jnp.dot(h, w_in_ref[:, sec * W_A:(sec + 1) * W_A], preferred_element_type=F32)


def _prompt_layer_kernel(x_ref, ng_ref, w_in_ref, lg_ref, lb_ref, ws_ref, bs_ref, wp_ref,
                         ps_ref, w_out_ref, fg_ref, y_ref, pool_ref, xb_ref, a_ref, *,
                         tile, sub, final_norm):
    j = pl.program_id(1)

    @pl.when(j == 0)
    def _():
        xb_ref[0:HALO, :] = jnp.zeros((HALO, W_B), F32)

    wm = _masked_spatial(ws_ref[...])
    bs = bs_ref[...]

    for s in range(tile // sub):
        rows = slice(s * sub, (s + 1) * sub)
        r0 = HALO + s * sub
        x = x_ref[rows, :]
        h = _rmsnorm(x, ng_ref[...]).astype(BF16)

        v = _project(h, w_in_ref, 1)
        xb_ref[r0:r0 + sub, :] = _project(h, w_in_ref, 3)
        u = _project(h, w_in_ref, 0)

        pm = _group_linear(_pool_delta(xb_ref, r0, sub, j * tile + s * sub), wp_ref)

        vn = _layernorm(v, lg_ref[...], lb_ref[...]).astype(BF16)
        sp = [[jnp.dot(wm[hd], vn[c * MLP_CHUNK:(c + 1) * MLP_CHUNK,
                                  hd * HEAD_DIM_A:(hd + 1) * HEAD_DIM_A],
                       preferred_element_type=F32) + bs[:, hd:hd + 1]
               for hd in range(HEADS_A)] for c in range(sub // MLP_CHUNK)]

        gate_a = _silu(_project(h, w_in_ref, 2))
        gate_b = _silu(_project(h, w_in_ref, 4))
        for c in range(sub // MLP_CHUNK):
            rs = slice(c * MLP_CHUNK, (c + 1) * MLP_CHUNK)
            for hd in range(HEADS_A):
                cs = slice(hd * HEAD_DIM_A, (hd + 1) * HEAD_DIM_A)
                a_ref[s * sub + c * MLP_CHUNK:s * sub + (c + 1) * MLP_CHUNK, cs] = (
                    u[rs, cs] * sp[c][hd] * gate_a[rs, cs]).astype(BF16)
        bb = (pm * ps_ref[...] * gate_b).astype(BF16)

        y = x + (jnp.dot(a_ref[rows, :], w_out_ref[0:W_A, :], preferred_element_type=F32)
                 + jnp.dot(bb, w_out_ref[W_A:W_A + W_B, :], preferred_element_type=F32))
        if final_norm:
            y = _rmsnorm(y, fg_ref[...])
        y_ref[rows, :] = y

    pool_ref[...] = xb_ref[pl.ds(HALO + tile - POOL_STATE, POOL_STATE), :]
    xb_ref[0:HALO, :] = xb_ref[tile:tile + HALO, :]


def _sample_layer_kernel(x_ref, st_ref, ng_ref, w_in_ref, lg_ref, lb_ref, ws_ref, bs_ref, wp_ref,
                         ps_ref, w_out_ref, fg_ref, y_ref, pool_ref, vn_ref, xb_ref, *,
                         n_batch, seq, pos0, final_norm):
    x = x_ref[...]
    h = _rmsnorm(x, ng_ref[...]).astype(BF16)

    vn = _layernorm(_project(h, w_in_ref, 1), lg_ref[...], lb_ref[...])
    vn_ref[...] = vn
    u = _project(h, w_in_ref, 0)
    gate_a = _silu(_project(h, w_in_ref, 2))
    wm = _masked_spatial(ws_ref[...])
    bs = bs_ref[...]
    pad = jnp.zeros((MLP_CHUNK - seq, W_A), F32)
    sp_rows = []
    for b in range(n_batch):
        vb = jnp.concatenate([vn[b * seq:(b + 1) * seq, :], pad], axis=0).astype(BF16)
        heads = []
        for hd in range(HEADS_A):
            cs = slice(hd * HEAD_DIM_A, (hd + 1) * HEAD_DIM_A)
            sp = jnp.dot(wm[hd], vb[:, cs], preferred_element_type=F32) + bs[:, hd:hd + 1]
            heads.append(sp[0:seq, :])
        sp_rows.append(jnp.concatenate(heads, axis=-1))
    a = (u * jnp.concatenate(sp_rows, axis=0) * gate_a).astype(BF16)

    xb = _project(h, w_in_ref, 3)
    stride = HALO + seq
    d_rows = []
    for b in range(n_batch):
        base = b * stride
        xb_ref[base + HALO - POOL_STATE:base + HALO, :] = st_ref[b]
        xb_ref[base + HALO:base + stride, :] = xb[b * seq:(b + 1) * seq, :]
        pool_ref[b] = xb_ref[pl.ds(base + stride - POOL_STATE, POOL_STATE), :]
        d_rows.append(_pool_delta(xb_ref, base + HALO, seq, pos0))
    d_parts = [jnp.concatenate([d_rows[b][g] for b in range(n_batch)], axis=0)
               for g in range(len(POOL_WINDOWS))]
    pm = _group_linear(d_parts, wp_ref)
    bb = (pm * ps_ref[...] * _silu(_project(h, w_in_ref, 4))).astype(BF16)

    y = x + (jnp.dot(a, w_out_ref[0:W_A, :], preferred_element_type=F32)
             + jnp.dot(bb, w_out_ref[W_A:W_A + W_B, :], preferred_element_type=F32))
    if final_norm:
        y = _rmsnorm(y, fg_ref[...])
    y_ref[...] = y


def _resident(shape):
    zeros = (0,) * len(shape)
    return pl.BlockSpec(shape, lambda b, j: zeros, pipeline_mode=pl.Buffered(1))


def _prompt_layer(x, params, final_norm):
    n_batch, seq, d = x.shape
    tile, sub = PROMPT_TILE, PROMPT_SUB
    assert seq % tile == 0 and tile % sub == 0 and sub % MLP_CHUNK == 0
    row_spec = pl.BlockSpec((None, tile, d), lambda b, j: (b, j, 0))
    in_specs = [row_spec] + [_resident(p.shape) for p in params]
    out_specs = [row_spec, pl.BlockSpec((None, POOL_STATE, W_B), lambda b, j: (b, 0, 0))]
    out_shape = [jax.ShapeDtypeStruct((n_batch, seq, d), F32),
                 jax.ShapeDtypeStruct((n_batch, POOL_STATE, W_B), F32)]
    return pl.pallas_call(
        functools.partial(_prompt_layer_kernel, tile=tile, sub=sub, final_norm=final_norm),
        grid=(n_batch, seq // tile),
        in_specs=in_specs,
        out_specs=out_specs,
        out_shape=out_shape,
        scratch_shapes=[pltpu.VMEM((HALO + tile, W_B), F32), pltpu.VMEM((tile, W_A), BF16)],
        compiler_params=pltpu.CompilerParams(
            dimension_semantics=("arbitrary", "arbitrary"),
            vmem_limit_bytes=VMEM_LIMIT_BYTES),
        name="prompt_layer",
    )(x, *params)


def _sample_layer(x, state, params, final_norm):
    n_batch, seq, d = x.shape
    assert POOL_STATE <= seq <= MLP_CHUNK and seq % SUBLANES == 0
    rows = n_batch * seq
    out_shape = [jax.ShapeDtypeStruct((rows, d), F32),
                 jax.ShapeDtypeStruct((n_batch, POOL_STATE, W_B), F32),
                 jax.ShapeDtypeStruct((rows, W_A), F32)]
    y, pool, vn = pl.pallas_call(
        functools.partial(_sample_layer_kernel, n_batch=n_batch, seq=seq, pos0=PAST_LEN,
                          final_norm=final_norm),
        out_shape=out_shape,
        scratch_shapes=[pltpu.VMEM((n_batch * (HALO + seq), W_B), F32)],
        compiler_params=pltpu.CompilerParams(vmem_limit_bytes=VMEM_LIMIT_BYTES),
        name="sample_layer",
    )(x.reshape(rows, d), state, *params)
    return y.reshape(n_batch, seq, d), pool, vn.reshape(n_batch, seq, W_A)


def kernel(x_prompt, x_sample, state_pool, norm_g, w_in, ln_v_g, ln_v_b, w_spatial, b_spatial,
           w_pool, pool_scale, w_out, final_g):
    depth = w_in.shape[0]
    w_in_b = w_in.astype(BF16)
    w_pool_b = w_pool.astype(BF16)
    w_out_b = w_out.astype(BF16)
    fg = final_g.reshape(1, D_MODEL)
    xp, xs = x_prompt, x_sample
    pool_p, pool_s, v_s = [], [], []
    for l in range(depth):
        params = (norm_g[l].reshape(1, D_MODEL), w_in_b[l], ln_v_g[l].reshape(1, W_A),
                  ln_v_b[l].reshape(1, W_A), w_spatial[l], b_spatial[l].T, w_pool_b[l],
                  pool_scale[l].reshape(1, W_B), w_out_b[l], fg)
        last = l == depth - 1
        xp, new_pool_p = _prompt_layer(xp, params, last)
        xs, new_pool_s, v_rows = _sample_layer(xs, state_pool[l], params, last)
        pool_p.append(new_pool_p)
        pool_s.append(new_pool_s)
        v_s.append(v_rows)
    return (xp, xs, jnp.stack(pool_p, axis=0), jnp.stack(pool_s, axis=0), jnp.stack(v_s, axis=0))
```

```python
import functools

import jax
import jax.numpy as jnp
from jax import lax
from jax.experimental import pallas as pl
from jax.experimental.pallas import tpu as pltpu

D_MODEL = 1024
W_A = 1024
W_B = 1024
HEADS_A = 8
HEAD_DIM_A = W_A // HEADS_A
CHUNK = 64
MLP_CHUNK = 128
POOL_WINDOWS = (2, 4, 8, 16)
POOL_GROUP_DIM = W_B // len(POOL_WINDOWS)
POOL_STATE = max(POOL_WINDOWS) - 1
PAST_LEN = 4096
EPS = 1e-6

LANES = 128
SUBLANES = 8
HALO = 16
VMEM_LIMIT_BYTES = 58 * 1024 * 1024
PROMPT_TILE = 1024
PROMPT_SUB = 512

F32 = jnp.float32
BF16 = jnp.bfloat16


def _rmsnorm(x, g):
    ms = jnp.mean(x * x, axis=-1, keepdims=True)
    return x * lax.rsqrt(ms + EPS) * g


def _layernorm(v, g, b):
    mu = jnp.mean(v, axis=-1, keepdims=True)
    vc = v - mu
    var = jnp.mean(vc * vc, axis=-1, keepdims=True)
    return vc * lax.rsqrt(var + EPS) * g + b


def _silu(z):
    return z / (1.0 + jnp.exp(-z))


def _masked_spatial(ws):
    i = lax.broadcasted_iota(jnp.int32, (MLP_CHUNK, MLP_CHUNK), 0)
    j = lax.broadcasted_iota(jnp.int32, (MLP_CHUNK, MLP_CHUNK), 1)
    keep = (i >= CHUNK) | (j < CHUNK)
    return jnp.where(keep[None], ws, 0.0).astype(BF16)


def _pool_delta(xb_ref, r0, rows, pos0):
    parts = []
    for g, w in enumerate(POOL_WINDOWS):
        cols = slice(g * POOL_GROUP_DIM, (g + 1) * POOL_GROUP_DIM)
        xe = xb_ref[pl.ds(r0 - HALO, HALO + rows), cols]
        s = xe
        span = 1
        while span < w:
            s = s + pltpu.roll(s, span, axis=0)
            span *= 2
        d = s[HALO:, :] * (1.0 / w) - xe[HALO:, :]
        if not (isinstance(pos0, int) and pos0 >= POOL_STATE):
            pos1 = lax.broadcasted_iota(jnp.int32, (HALO, POOL_GROUP_DIM), 0) + (pos0 + 1)
            inv = 1.0 / jnp.minimum(pos1, w).astype(F32)
            head = s[HALO:2 * HALO, :] * inv - xe[HALO:2 * HALO, :]
            d = jnp.concatenate([head, d[HALO:, :]], axis=0)
        parts.append(d)
    return parts


def _group_linear(d_parts, wp_ref):
    outs = [jnp.dot(d.astype(BF16), wp_ref[g], preferred_element_type=F32)
            for g, d in enumerate(d_parts)]
    return jnp.concatenate(outs, axis=-1)


def _project(h, w_in_ref, sec):
    return jnp.dot(h, w_in_ref[:, sec * W_A:(sec + 1) * W_A], preferred_element_type=F32)


def _prompt_layer_kernel(x_ref, ng_ref, w_in_ref, lg_ref, lb_ref, ws_ref, bs_ref, wp_ref,
                         ps_ref, w_out_ref, fg_ref, y_ref, pool_ref, xb_ref, a_ref, *,
                         tile, sub, final_norm):
    j = pl.program_id(1)

    @pl.when(j == 0)
    def _():
        xb_ref[0:HALO, :] = jnp.zeros((HALO, W_B), F32)

    wm = _masked_spatial(ws_ref[...])
    bs = bs_ref[...]

    for s in range(tile // sub):
        rows = slice(s * sub, (s + 1) * sub)
        r0 = HALO + s * sub
        x = x_ref[rows, :]
        h = _rmsnorm(x, ng_ref[...]).astype(BF16)

        v = _project(h, w_in_ref, 1)
        xb_ref[r0:r0 + sub, :] = _project(h, w_in_ref, 3)
        u = _project(h, w_in_ref, 0)

        pm = _group_linear(_pool_delta(xb_ref, r0, sub, j * tile + s * sub), wp_ref)

        vn = _layernorm(v, lg_ref[...], lb_ref[...]).astype(BF16)
        sp = [[jnp.dot(wm[hd], vn[c * MLP_CHUNK:(c + 1) * MLP_CHUNK,
                                  hd * HEAD_DIM_A:(hd + 1) * HEAD_DIM_A],
                       preferred_element_type=F32) + bs[:, hd:hd + 1]
               for hd in range(HEADS_A)] for c in range(sub // MLP_CHUNK)]

        gate_a = _silu(_project(h, w_in_ref, 2))
        gate_b = _silu(_project(h, w_in_ref, 4))
        for c in range(sub // MLP_CHUNK):
            rs = slice(c * MLP_CHUNK, (c + 1) * MLP_CHUNK)
            for hd in range(HEADS_A):
                cs = slice(hd * HEAD_DIM_A, (hd + 1) * HEAD_DIM_A)
                a_ref[s * sub + c * MLP_CHUNK:s * sub + (c + 1) * MLP_CHUNK, cs] = (
                    u[rs, cs] * sp[c][hd] * gate_a[rs, cs]).astype(BF16)
        bb = (pm * ps_ref[...] * gate_b).astype(BF16)

        y = x + (jnp.dot(a_ref[rows, :], w_out_ref[0:W_A, :], preferred_element_type=F32)
                 + jnp.dot(bb, w_out_ref[W_A:W_A + W_B, :], preferred_element_type=F32))
        if final_norm:
            y = _rmsnorm(y, fg_ref[...])
        y_ref[rows, :] = y

    pool_ref[...] = xb_ref[pl.ds(HALO + tile - POOL_STATE, POOL_STATE), :]
    xb_ref[0:HALO, :] = xb_ref[tile:tile + HALO, :]


def _sample_layer_kernel(x_ref, st_ref, ng_ref, w_in_ref, lg_ref, lb_ref, ws_ref, bs_ref, wp_ref,
                         ps_ref, w_out_ref, fg_ref, y_ref, pool_ref, vn_ref, xb_ref, *,
                         n_batch, seq, pos0, final_norm):
    x = x_ref[...]
    h = _rmsnorm(x, ng_ref[...]).astype(BF16)

    vn = _layernorm(_project(h, w_in_ref, 1), lg_ref[...], lb_ref[...])
    vn_ref[...] = vn
    u = _project(h, w_in_ref, 0)
    gate_a = _silu(_project(h, w_in_ref, 2))
    wm = _masked_spatial(ws_ref[...])
    bs = bs_ref[...]
    pad = jnp.zeros((MLP_CHUNK - seq, W_A), F32)
    sp_rows = []
    for b in range(n_batch):
        vb = jnp.concatenate([vn[b * seq:(b + 1) * seq, :], pad], axis=0).astype(BF16)
        heads = []
        for hd in range(HEADS_A):
            cs = slice(hd * HEAD_DIM_A, (hd + 1) * HEAD_DIM_A)
            sp = jnp.dot(wm[hd], vb[:, cs], preferred_element_type=F32) + bs[:, hd:hd + 1]
            heads.append(sp[0:seq, :])
        sp_rows.append(jnp.concatenate(heads, axis=-1))
    a = (u * jnp.concatenate(sp_rows, axis=0) * gate_a).astype(BF16)

    xb = _project(h, w_in_ref, 3)
    stride = HALO + seq
    d_rows = []
    for b in range(n_batch):
        base = b * stride
        xb_ref[base + HALO - POOL_STATE:base + HALO, :] = st_ref[b]
        xb_ref[base + HALO:base + stride, :] = xb[b * seq:(b + 1) * seq, :]
        pool_ref[b] = xb_ref[pl.ds(base + stride - POOL_STATE, POOL_STATE), :]
        d_rows.append(_pool_delta(xb_ref, base + HALO, seq, pos0))
    d_parts = [jnp.concatenate([d_rows[b][g] for b in range(n_batch)], axis=0)
               for g in range(len(POOL_WINDOWS))]
    pm = _group_linear(d_parts, wp_ref)
    bb = (pm * ps_ref[...] * _silu(_project(h, w_in_ref, 4))).astype(BF16)

    y = x + (jnp.dot(a, w_out_ref[0:W_A, :], preferred_element_type=F32)
             + jnp.dot(bb, w_out_ref[W_A:W_A + W_B, :], preferred_element_type=F32))
    if final_norm:
        y = _rmsnorm(y, fg_ref[...])
    y_ref[...] = y


def _layer_spec(stacked, layer):
    index = (layer,) + (0,) * (stacked.ndim - 1)
    return pl.BlockSpec((None,) + stacked.shape[1:], lambda *_: index,
                        pipeline_mode=pl.Buffered(1))


def _prompt_layer(x, params, final_g, layer, final_norm):
    n_batch, seq, d = x.shape
    tile, sub = PROMPT_TILE, PROMPT_SUB
    assert seq % tile == 0 and tile % sub == 0 and sub % MLP_CHUNK == 0
    row_spec = pl.BlockSpec((None, tile, d), lambda b, j: (b, j, 0))
    in_specs = [row_spec] + [_layer_spec(p, layer) for p in params] + [_layer_spec(final_g, 0)]
    out_specs = [row_spec, pl.BlockSpec((None, POOL_STATE, W_B), lambda b, j: (b, 0, 0))]
    out_shape = [jax.ShapeDtypeStruct((n_batch, seq, d), F32),
                 jax.ShapeDtypeStruct((n_batch, POOL_STATE, W_B), F32)]
    return pl.pallas_call(
        functools.partial(_prompt_layer_kernel, tile=tile, sub=sub, final_norm=final_norm),
        grid=(n_batch, seq // tile),
        in_specs=in_specs,
        out_specs=out_specs,
        out_shape=out_shape,
        scratch_shapes=[pltpu.VMEM((HALO + tile, W_B), F32), pltpu.VMEM((tile, W_A), BF16)],
        compiler_params=pltpu.CompilerParams(
            dimension_semantics=("arbitrary", "arbitrary"),
            vmem_limit_bytes=VMEM_LIMIT_BYTES),
        name="prompt_layer",
    )(x, *params, final_g)


def _sample_layer(x, state, params, final_g, layer, final_norm):
    n_batch, seq, d = x.shape
    assert POOL_STATE <= seq <= MLP_CHUNK and seq % SUBLANES == 0
    rows = n_batch * seq
    whole = lambda shape: pl.BlockSpec(shape, lambda *_: (0,) * len(shape))
    out_shape = [jax.ShapeDtypeStruct((rows, d), F32),
                 jax.ShapeDtypeStruct((n_batch, POOL_STATE, W_B), F32),
                 jax.ShapeDtypeStruct((rows, W_A), F32)]
    y, pool, vn = pl.pallas_call(
        functools.partial(_sample_layer_kernel, n_batch=n_batch, seq=seq, pos0=PAST_LEN,
                          final_norm=final_norm),
        grid=(1,),
        in_specs=([whole((rows, d)), _layer_spec(state, layer)]
                  + [_layer_spec(p, layer) for p in params] + [_layer_spec(final_g, 0)]),
        out_specs=[whole(s.shape) for s in out_shape],
        out_shape=out_shape,
        scratch_shapes=[pltpu.VMEM((n_batch * (HALO + seq), W_B), F32)],
        compiler_params=pltpu.CompilerParams(
            dimension_semantics=("arbitrary",), vmem_limit_bytes=VMEM_LIMIT_BYTES),
        name="sample_layer",
    )(x.reshape(rows, d), state, *params, final_g)
    return y.reshape(n_batch, seq, d), pool, vn.reshape(n_batch, seq, W_A)


def kernel(x_prompt, x_sample, state_pool, norm_g, w_in, ln_v_g, ln_v_b, w_spatial, b_spatial,
           w_pool, pool_scale, w_out, final_g):
    depth = w_in.shape[0]
    params = (norm_g.reshape(depth, 1, D_MODEL), w_in.astype(BF16),
              ln_v_g.reshape(depth, 1, W_A), ln_v_b.reshape(depth, 1, W_A), w_spatial,
              jnp.swapaxes(b_spatial, 1, 2), w_pool.astype(BF16),
              pool_scale.reshape(depth, 1, W_B), w_out.astype(BF16))
    fg = final_g.reshape(1, 1, D_MODEL)
    xp, xs = x_prompt, x_sample
    pool_p, pool_s, v_s = [], [], []
    for l in range(depth):
        last = l == depth - 1
        xp, new_pool_p = _prompt_layer(xp, params, fg, l, last)
        xs, new_pool_s, v_rows = _sample_layer(xs, state_pool, params, fg, l, last)
        pool_p.append(new_pool_p)
        pool_s.append(new_pool_s)
        v_s.append(v_rows)
    return (xp, xs, jnp.stack(pool_p, axis=0), jnp.stack(pool_s, axis=0), jnp.stack(v_s, axis=0))
```
